```python
import jax, jax.numpy as jnp
from jax import lax
import numpy as np

D_MODEL = 1024
BATCH = 4
SEQ = 8192
DEPTH = 1

HEAD_DIM = 64
SB_HEADS = 8
SWA_HEADS = 8
SWA_KV_HEADS = 2
SWA_GROUP = SWA_HEADS // SWA_KV_HEADS
SB_WIDTH = SB_HEADS * HEAD_DIM
SWA_Q_WIDTH = SWA_HEADS * HEAD_DIM
SWA_KV_WIDTH = SWA_KV_HEADS * HEAD_DIM
MIX_WIDTH = SB_WIDTH + SWA_Q_WIDTH
IN_PROJ_WIDTH = 3 * SB_WIDTH + SWA_Q_WIDTH + 2 * SWA_KV_WIDTH
WINDOW = 128
BLOCK = 128
ROPE_THETA = 10000.0
N_KEYS = 128
N_EXPERTS = N_KEYS * N_KEYS
PEER_HEADS = 8
PEER_QUERY_DIM = 256
PEER_HALF = PEER_QUERY_DIM // 2
PEER_TOPK = 16
PEER_CHUNK = 128
EPS = 1e-6

kernel_name = "hymba_stickbreak_swa_sink_peer"


def rms_norm(x, g):
    xf = x.astype(jnp.float32)
    y = xf * lax.rsqrt(jnp.mean(xf * xf, axis=-1, keepdims=True) + EPS)
    return (y * g.astype(jnp.float32)).astype(x.dtype)


def rope(x, positions):
    half = HEAD_DIM // 2
    inv_freq = ROPE_THETA ** (-jnp.arange(half, dtype=jnp.float32) / half)
    ang = positions.astype(jnp.float32)[..., None] * inv_freq
    cos = jnp.cos(ang)[:, :, None, :]
    sin = jnp.sin(ang)[:, :, None, :]
    xf = x.astype(jnp.float32)
    x1, x2 = xf[..., :half], xf[..., half:]
    out = jnp.concatenate([x1 * cos - x2 * sin, x2 * cos + x1 * sin], axis=-1)
    return out.astype(x.dtype)


def stick_breaking_attention(q, k, v):
    B, S, H, D = q.shape
    nblk = S // BLOCK
    scale = D ** -0.5
    qb = q.reshape(B, nblk, BLOCK, H, D).transpose(1, 0, 3, 2, 4)
    kh = k.transpose(0, 2, 1, 3)
    vh = v.transpose(0, 2, 1, 3)
    key_pos = jnp.arange(S)

    def one_block(args):
        i, qi = args
        q_pos = i * BLOCK + jnp.arange(BLOCK)
        z = jnp.einsum('bhqd,bhkd->bhqk', qi, kh).astype(jnp.float32) * scale
        causal = key_pos[None, :] < q_pos[:, None]
        log_one_minus = jnp.where(causal, -jax.nn.softplus(z), 0.0)
        after = lax.cumsum(log_one_minus, axis=3, reverse=True) - log_one_minus
        w = jnp.where(causal, jnp.exp(jax.nn.log_sigmoid(z) + after), 0.0)
        return jnp.einsum('bhqk,bhkd->bhqd', w.astype(v.dtype), vh)

    out = lax.map(one_block, (jnp.arange(nblk), qb))
    return out.transpose(1, 0, 3, 2, 4).reshape(B, S, H, D)


def sliding_window_sink_attention(q, k, v, sinks):
    B, S, Hq, D = q.shape
    nblk = S // BLOCK
    scale = D ** -0.5
    qb = q.reshape(B, nblk, BLOCK, SWA_KV_HEADS, SWA_GROUP, D)
    kb = k.reshape(B, nblk, BLOCK, SWA_KV_HEADS, D)
    vb = v.reshape(B, nblk, BLOCK, SWA_KV_HEADS, D)
    pad_k = jnp.zeros_like(kb[:, :1])
    pad_v = jnp.zeros_like(vb[:, :1])
    kw = jnp.concatenate([jnp.concatenate([pad_k, kb[:, :-1]], axis=1), kb], axis=2)
    vw = jnp.concatenate([jnp.concatenate([pad_v, vb[:, :-1]], axis=1), vb], axis=2)
    z = jnp.einsum('bnqhgd,bnkhd->bnhgqk', qb, kw).astype(jnp.float32) * scale
    q_idx = jnp.arange(BLOCK)[:, None] + BLOCK
    k_idx = jnp.arange(2 * BLOCK)[None, :]
    diff = q_idx - k_idx
    band = (diff >= 0) & (diff < WINDOW)
    not_pad = (jnp.arange(nblk)[:, None, None] > 0) | (k_idx[None] >= BLOCK)
    mask = band[None] & not_pad
    z = jnp.where(mask[None, :, None, None], z, -jnp.inf)
    sink = sinks.astype(jnp.float32).reshape(SWA_KV_HEADS, SWA_GROUP)[None, None, :, :, None, None]
    sink = jnp.broadcast_to(sink, z.shape[:-1] + (1,))
    p = jax.nn.softmax(jnp.concatenate([z, sink], axis=-1), axis=-1)[..., :-1]
    out = jnp.einsum('bnhgqk,bnkhd->bnqhgd', p.astype(v.dtype), vw)
    return out.reshape(B, S, Hq, D)


def peer_ffn(x, w_query, sub_keys_1, sub_keys_2, expert_down, expert_up):
    B, S, D = x.shape
    T = B * S
    xt = x.reshape(T // PEER_CHUNK, PEER_CHUNK, D)

    def one_chunk(xc):
        q = (xc @ w_query).reshape(PEER_CHUNK, PEER_HEADS, PEER_QUERY_DIM)
        q1, q2 = q[..., :PEER_HALF], q[..., PEER_HALF:]
        s1 = jnp.einsum('thd,hkd->thk', q1, sub_keys_1).astype(jnp.float32)
        s2 = jnp.einsum('thd,hkd->thk', q2, sub_keys_2).astype(jnp.float32)
        v1, i1 = lax.top_k(s1, PEER_TOPK)
        v2, i2 = lax.top_k(s2, PEER_TOPK)
        cand = (v1[..., :, None] + v2[..., None, :]).reshape(PEER_CHUNK, PEER_HEADS, PEER_TOPK * PEER_TOPK)
        cand_idx = (i1[..., :, None] * N_KEYS + i2[..., None, :]).reshape(PEER_CHUNK, PEER_HEADS, PEER_TOPK * PEER_TOPK)
        top_s, pos = lax.top_k(cand, PEER_TOPK)
        idx = jnp.take_along_axis(cand_idx, pos, axis=-1)
        gate = jax.nn.softmax(top_s, axis=-1)
        u = expert_down[idx]
        hidden = jax.nn.gelu(jnp.einsum('thkd,td->thk', u, xc).astype(jnp.float32), approximate=False)
        w = (gate * hidden).astype(x.dtype)
        vsel = expert_up[idx]
        return jnp.einsum('thk,thkd->td', w, vsel)

    out = lax.map(one_chunk, xt)
    return out.reshape(B, S, D)


def setup_inputs(seed: int = 0) -> dict:
    key = jax.random.key(seed)
    ks = jax.random.split(key, 16)
    f32 = jnp.float32
    x = jax.random.normal(ks[0], (BATCH, SEQ, D_MODEL), f32)
    offsets = jax.random.randint(ks[1], (BATCH,), 0, 1024, dtype=jnp.int32)
    positions = offsets[:, None] + jnp.arange(SEQ, dtype=jnp.int32)[None, :]

    def gain(k, n):
        return 1.0 + 0.02 * jax.random.normal(k, (DEPTH, n), f32)

    return {
        "x": x,
        "positions": positions,
        "attn_norm": gain(ks[2], D_MODEL),
        "w_in": jax.random.normal(ks[3], (DEPTH, D_MODEL, IN_PROJ_WIDTH), f32) * D_MODEL ** -0.5,
        "sb_out_norm": gain(ks[4], SB_WIDTH),
        "swa_sinks": 0.5 * jax.random.normal(ks[5], (DEPTH, SWA_HEADS), f32),
        "swa_out_norm": gain(ks[6], SWA_Q_WIDTH),
        "w_out": jax.random.normal(ks[7], (DEPTH, MIX_WIDTH, D_MODEL), f32) * MIX_WIDTH ** -0.5,
        "ffn_norm": gain(ks[8], D_MODEL),
        "peer_w_query": jax.random.normal(ks[9], (DEPTH, D_MODEL, PEER_HEADS * PEER_QUERY_DIM), f32) * D_MODEL ** -0.5,
        "peer_sub_keys_1": jax.random.normal(ks[10], (DEPTH, PEER_HEADS, N_KEYS, PEER_HALF), f32) * PEER_HALF ** -0.5,
        "peer_sub_keys_2": jax.random.normal(ks[11], (DEPTH, PEER_HEADS, N_KEYS, PEER_HALF), f32) * PEER_HALF ** -0.5,
        "peer_expert_down": jax.random.normal(ks[12], (DEPTH, N_EXPERTS, D_MODEL), f32) * D_MODEL ** -0.5,
        "peer_expert_up": 0.5 * jax.random.normal(ks[13], (DEPTH, N_EXPERTS, D_MODEL), f32),
        "final_norm": 1.0 + 0.02 * jax.random.normal(ks[14], (D_MODEL,), f32),
    }


def reference(x, positions, attn_norm, w_in, sb_out_norm, swa_sinks, swa_out_norm, w_out,
              ffn_norm, peer_w_query, peer_sub_keys_1, peer_sub_keys_2, peer_expert_down,
              peer_expert_up, final_norm):
    B, S, _ = x.shape
    splits = np.cumsum([SB_WIDTH, SB_WIDTH, SB_WIDTH, SWA_Q_WIDTH, SWA_KV_WIDTH]).tolist()
    for layer in range(DEPTH):
        h = rms_norm(x, attn_norm[layer])
        proj = h @ w_in[layer]
        sb_q, sb_k, sb_v, sw_q, sw_k, sw_v = jnp.split(proj, splits, axis=-1)
        sb_q = sb_q.reshape(B, S, SB_HEADS, HEAD_DIM)
        sb_k = sb_k.reshape(B, S, SB_HEADS, HEAD_DIM)
        sb_v = sb_v.reshape(B, S, SB_HEADS, HEAD_DIM)
        sw_q = rope(sw_q.reshape(B, S, SWA_HEADS, HEAD_DIM), positions)
        sw_k = rope(sw_k.reshape(B, S, SWA_KV_HEADS, HEAD_DIM), positions)
        sw_v = sw_v.reshape(B, S, SWA_KV_HEADS, HEAD_DIM)
        sb_o = stick_breaking_attention(sb_q, sb_k, sb_v).reshape(B, S, SB_WIDTH)
        sw_o = sliding_window_sink_attention(sw_q, sw_k, sw_v, swa_sinks[layer]).reshape(B, S, SWA_Q_WIDTH)
        mix = jnp.concatenate([rms_norm(sb_o, sb_out_norm[layer]),
                               rms_norm(sw_o, swa_out_norm[layer])], axis=-1)
        x = x + mix @ w_out[layer]
        x = x + peer_ffn(rms_norm(x, ffn_norm[layer]), peer_w_query[layer],
                         peer_sub_keys_1[layer], peer_sub_keys_2[layer],
                         peer_expert_down[layer], peer_expert_up[layer])
    return rms_norm(x, final_norm)
```

```python
import functools
import math

import jax
import jax.numpy as jnp
from jax import lax
from jax.experimental import pallas as pl
from jax.experimental.pallas import tpu as pltpu

D_MODEL = 1024
HEAD_DIM = 64
SB_WIDTH = 512
SWA_Q_WIDTH = 512
SWA_KV_HEADS = 2
SWA_HEADS = 8
BLOCK = 128
ROPE_THETA = 10000.0
N_KEYS = 128
N_EXPERTS = N_KEYS * N_KEYS
PEER_HEADS = 8
PEER_HALF = 128
PEER_TOPK = 16
SLOTS = PEER_HEADS * PEER_TOPK
EPS = 1e-6
ATTN_SCALE = HEAD_DIM ** -0.5

LANES = 128
SUBLANES = 8
VMEM_LIMIT = 56 * 1024 * 1024

PROJ_TILE = 512
ROUTE_TILE = 256
TOK_TILE = 128
TOK_GROUP = SUBLANES
ROW_WORDS = D_MODEL // 2
ROW_SUB = ROW_WORDS // LANES
GSTRIDE = SLOTS + SUBLANES

SB_EXIT = 104.0

F32 = jnp.float32
BF16 = jnp.bfloat16


def _rms(v, g):
    return v * lax.rsqrt(jnp.mean(v * v, axis=-1, keepdims=True) + EPS) * g


def _split_bf16(v):
    hi = v.astype(BF16)
    return hi, (v - hi.astype(F32)).astype(BF16)


def _dot_t(a, b):
    return lax.dot_general(a, b, (((1,), (1,)), ((), ())), preferred_element_type=F32)


def _dot(a, b):
    return jnp.dot(a, b, preferred_element_type=F32)


def _proj_kernel(x_ref, pos_ref, g_ref, w_ref, rope_ref, sbq_ref, sbk_ref, sbv_ref, swq_ref, swk_ref, swv_ref):
    h = _rms(x_ref[...], g_ref[...]).astype(BF16)
    ang = pos_ref[...].astype(F32) * rope_ref[0:1, :]
    sign = rope_ref[1:2, :]
    cos = jnp.cos(ang)
    sin = jnp.sin(ang) * sign
    first_half = sign < 0

    def rope(v):
        partner = jnp.where(first_half, pltpu.roll(v, LANES - HEAD_DIM // 2, 1), pltpu.roll(v, HEAD_DIM // 2, 1))
        return v * cos + partner * sin

    def proj(c0, n):
        return _dot(h, w_ref[:, c0:c0 + n])

    sbq_ref[...] = (proj(0, SB_WIDTH) * ATTN_SCALE).astype(BF16)
    sbk_ref[...] = proj(SB_WIDTH, SB_WIDTH).astype(BF16)
    sbv_ref[...] = proj(2 * SB_WIDTH, SB_WIDTH).astype(BF16)
    c0 = 3 * SB_WIDTH
    for c in range(SWA_Q_WIDTH // LANES):
        swq_ref[:, c * LANES:(c + 1) * LANES] = (rope(proj(c0 + c * LANES, LANES)) * ATTN_SCALE).astype(BF16)
    c0 += SWA_Q_WIDTH
    for g in range(SWA_KV_HEADS):
        swk_ref[:, g * LANES:(g + 1) * LANES] = rope(proj(c0 + g * LANES, LANES)).astype(BF16)
    c0 += SWA_KV_HEADS * LANES
    swv_ref[...] = proj(c0, SWA_KV_HEADS * LANES).astype(BF16)


def _proj(x2, pos2, g, w_ext, rope_tab):
    T = x2.shape[0]
    row = lambda i: (i, 0)
    fixed = lambda i: (0, 0)
    kvw = SWA_KV_HEADS * LANES
    return pl.pallas_call(
        _proj_kernel,
        grid=(T // PROJ_TILE,),
        in_specs=[pl.BlockSpec((PROJ_TILE, D_MODEL), row), pl.BlockSpec((PROJ_TILE, 1), row),
                  pl.BlockSpec((1, D_MODEL), fixed), pl.BlockSpec(w_ext.shape, fixed),
                  pl.BlockSpec(rope_tab.shape, fixed)],
        out_specs=[pl.BlockSpec((PROJ_TILE, SB_WIDTH), row)] * 3
        + [pl.BlockSpec((PROJ_TILE, SWA_Q_WIDTH), row), pl.BlockSpec((PROJ_TILE, kvw), row),
           pl.BlockSpec((PROJ_TILE, kvw), row)],
        out_shape=[jax.ShapeDtypeStruct((T, SB_WIDTH), BF16)] * 3
        + [jax.ShapeDtypeStruct((T, SWA_Q_WIDTH), BF16), jax.ShapeDtypeStruct((T, kvw), BF16),
           jax.ShapeDtypeStruct((T, kvw), BF16)],
        compiler_params=pltpu.CompilerParams(vmem_limit_bytes=VMEM_LIMIT, dimension_semantics=("arbitrary",)),
        name="proj",
    )(x2, pos2, g, w_ext, rope_tab)


def _sb_kernel(q_ref, k_ref, v_ref, o_ref):
    i = pl.program_id(2)
    q = q_ref[...]
    lane = lax.broadcasted_iota(jnp.int32, (BLOCK, LANES), 1)
    row = lax.broadcasted_iota(jnp.int32, (BLOCK, BLOCK), 0)
    col = lax.broadcasted_iota(jnp.int32, (BLOCK, BLOCK), 1)
    causal = col < row
    rj = lax.broadcasted_iota(jnp.int32, (BLOCK, 2 * BLOCK), 0)
    cj = lax.broadcasted_iota(jnp.int32, (BLOCK, 2 * BLOCK), 1)
    suffix = jnp.where((cj >= BLOCK) | (rj > cj), 1.0, 0.0).astype(BF16)

    def tile(qh, j, carry, diagonal):
        start = pl.multiple_of(j * BLOCK, BLOCK)
        k = k_ref[pl.ds(start, BLOCK), :]
        v = v_ref[pl.ds(start, BLOCK), :]
        z = _dot_t(qh, k)
        sp = jnp.maximum(z, 0.0) + jnp.log1p(jnp.exp(-jnp.abs(z)))
        lom = jnp.where(causal, -sp, 0.0) if diagonal else -sp
        hi, lo = _split_bf16(lom)
        sums = _dot(hi, suffix) + _dot(lo, suffix)
        w = jnp.exp(z - sp + sums[:, :BLOCK] + carry)
        if diagonal:
            w = jnp.where(causal, w, 0.0)
        return _dot(w.astype(BF16), v), carry + sums[:, BLOCK:]

    outs = []
    for h in range(2):
        qh = jnp.where((lane >= HEAD_DIM) == (h == 1), q, jnp.zeros_like(q))
        acc, carry = tile(qh, i, jnp.zeros((BLOCK, BLOCK), F32), True)

        def cond(s):
            return jnp.logical_and(s[0] >= 0, jnp.max(s[2]) > -SB_EXIT)

        def body(s, qh=qh):
            pv, carry = tile(qh, s[0], s[2], False)
            return s[0] - 1, s[1] + pv, carry

        _, acc, _ = lax.while_loop(cond, body, (i - 1, acc, carry))
        outs.append(acc)
    o_ref[...] = jnp.where(lane < HEAD_DIM, outs[0], outs[1])


def _sb_attention(q, k, v):
    B, S, _ = q.shape
    qmap = lambda b, p, i: (b, i, p)
    kvmap = lambda b, p, i: (b, 0, p)
    return pl.pallas_call(
        _sb_kernel,
        grid=(B, SB_WIDTH // LANES, S // BLOCK),
        in_specs=[pl.BlockSpec((None, BLOCK, LANES), qmap), pl.BlockSpec((None, S, LANES), kvmap),
                  pl.BlockSpec((None, S, LANES), kvmap)],
        out_specs=pl.BlockSpec((None, BLOCK, LANES), qmap),
        out_shape=jax.ShapeDtypeStruct((B, S, SB_WIDTH), F32),
        compiler_params=pltpu.CompilerParams(vmem_limit_bytes=VMEM_LIMIT,
                                             dimension_semantics=("arbitrary", "arbitrary", "arbitrary")),
        name="sb_attention",
    )(q, k, v)


def _swa_kernel(sink_ref, q_ref, kp_ref, kc_ref, vp_ref, vc_ref, o_ref):
    i = pl.program_id(1)
    r = lax.broadcasted_iota(jnp.int32, (BLOCK, 2 * BLOCK), 0)
    c = lax.broadcasted_iota(jnp.int32, (BLOCK, 2 * BLOCK), 1)
    valid = (c > r) & (c <= r + BLOCK) & ((c >= BLOCK) | (i > 0))
    lane = lax.broadcasted_iota(jnp.int32, (BLOCK, LANES), 1)
    heads_per_kv = SWA_HEADS // SWA_KV_HEADS
    for ch in range(SWA_Q_WIDTH // LANES):
        g = (2 * ch) // heads_per_kv
        sl = slice(g * LANES, (g + 1) * LANES)
        q = q_ref[:, ch * LANES:(ch + 1) * LANES]
        k = jnp.concatenate([kp_ref[:, sl], kc_ref[:, sl]], axis=0)
        v = jnp.concatenate([vp_ref[:, sl], vc_ref[:, sl]], axis=0)
        outs = []
        for hh in range(2):
            qh = jnp.where((lane >= HEAD_DIM) == (hh == 1), q, jnp.zeros_like(q))
            z = jnp.where(valid, _dot_t(qh, k), -1e30)
            sink = sink_ref[0, 2 * ch + hh]
            m = jnp.maximum(jnp.max(z, axis=1, keepdims=True), sink)
            e = jnp.exp(z - m)
            denom = jnp.sum(e, axis=1, keepdims=True) + jnp.exp(sink - m)
            outs.append(_dot((e / denom).astype(BF16), v))
        o_ref[:, ch * LANES:(ch + 1) * LANES] = jnp.where(lane < HEAD_DIM, outs[0], outs[1])


def _swa_attention(sinks, q, k, v):
    B, S, _ = q.shape
    kvw = SWA_KV_HEADS * LANES
    cur = lambda b, i: (b, i, 0)
    prev = lambda b, i: (b, jnp.maximum(i - 1, 0), 0)
    return pl.pallas_call(
        _swa_kernel,
        grid=(B, S // BLOCK),
        in_specs=[pl.BlockSpec((1, SWA_HEADS), lambda b, i: (0, 0), memory_space=pltpu.SMEM),
                  pl.BlockSpec((None, BLOCK, SWA_Q_WIDTH), cur),
                  pl.BlockSpec((None, BLOCK, kvw), prev), pl.BlockSpec((None, BLOCK, kvw), cur),
                  pl.BlockSpec((None, BLOCK, kvw), prev), pl.BlockSpec((None, BLOCK, kvw), cur)],
        out_specs=pl.BlockSpec((None, BLOCK, SWA_Q_WIDTH), cur),
        out_shape=jax.ShapeDtypeStruct((B, S, SWA_Q_WIDTH), F32),
        compiler_params=pltpu.CompilerParams(vmem_limit_bytes=VMEM_LIMIT,
                                             dimension_semantics=("arbitrary", "arbitrary")),
        name="swa_attention",
    )(sinks, q, k, k, v, v)


def _topk_rows(s, n, payload=None):
    rows = s.shape[0]
    iota = lax.broadcasted_iota(jnp.int32, s.shape, 0)
    vals, picks = [], []
    for _ in range(n):
        m = jnp.max(s, axis=0, keepdims=True)
        pos = jnp.min(jnp.where(s == m, iota, rows), axis=0, keepdims=True)
        hit = iota == pos
        vals.append(m)
        picks.append(pos if payload is None else jnp.max(jnp.where(hit, payload, -1), axis=0, keepdims=True))
        s = jnp.where(hit, -jnp.inf, s)
    return jnp.concatenate(vals, axis=0), jnp.concatenate(picks, axis=0)


def _route_kernel(x_ref, sbo_ref, swo_ref, gsb_ref, gsw_ref, wout_ref, gffn_ref, wq_ref,
                  k1h_ref, k1l_ref, k2h_ref, k2l_ref, x1_ref, h2_ref, idx_ref, gate_ref):
    sbn = _rms(sbo_ref[...], gsb_ref[...]).astype(BF16)
    swn = _rms(swo_ref[...], gsw_ref[...]).astype(BF16)
    x1 = x_ref[...] + _dot(sbn, wout_ref[0:SB_WIDTH, :]) + _dot(swn, wout_ref[SB_WIDTH:, :])
    x1_ref[...] = x1
    h2 = _rms(x1, gffn_ref[...])
    h2_ref[...] = h2
    q = _dot(h2.astype(BF16), wq_ref[...])

    def scores(qq, kh_ref, kl_ref, h):
        qh, ql = _split_bf16(qq)
        return _dot_t(kh_ref[h], qh) + _dot_t(kh_ref[h], ql) + _dot_t(kl_ref[h], qh)

    idx_rows, gate_rows = [], []
    for h in range(PEER_HEADS):
        c0 = h * 2 * PEER_HALF
        v1, i1 = _topk_rows(scores(q[:, c0:c0 + PEER_HALF], k1h_ref, k1l_ref, h), PEER_TOPK)
        v2, i2 = _topk_rows(scores(q[:, c0 + PEER_HALF:c0 + 2 * PEER_HALF], k2h_ref, k2l_ref, h), PEER_TOPK)
        cand = jnp.concatenate([v1[a:a + 1] + v2 for a in range(PEER_TOPK)], axis=0)
        cidx = jnp.concatenate([i1[a:a + 1] * N_KEYS + i2 for a in range(PEER_TOPK)], axis=0)
        top_s, idx = _topk_rows(cand, PEER_TOPK, payload=cidx)
        e = jnp.exp(top_s - top_s[0:1])
        idx_rows.append(idx)
        gate_rows.append(e / jnp.sum(e, axis=0, keepdims=True))
    idx_t = jnp.concatenate(idx_rows, axis=0).astype(F32)
    idx_ref[...] = idx_t.T.astype(jnp.int32)
    gate_ref[...] = jnp.concatenate(gate_rows, axis=0).T


def _route(x2, sbo, swo, gsb, gsw, wout, gffn, wq, k1h, k1l, k2h, k2l):
    T = x2.shape[0]
    row = lambda i: (i, 0)
    fixed2 = lambda i: (0, 0)
    fixed3 = lambda i: (0, 0, 0)
    keys = pl.BlockSpec(k1h.shape, fixed3)
    return pl.pallas_call(
        _route_kernel,
        grid=(T // ROUTE_TILE,),
        in_specs=[pl.BlockSpec((ROUTE_TILE, D_MODEL), row), pl.BlockSpec((ROUTE_TILE, SB_WIDTH), row),
                  pl.BlockSpec((ROUTE_TILE, SWA_Q_WIDTH), row),
                  pl.BlockSpec((1, SB_WIDTH), fixed2), pl.BlockSpec((1, SWA_Q_WIDTH), fixed2),
                  pl.BlockSpec(wout.shape, fixed2), pl.BlockSpec((1, D_MODEL), fixed2),
                  pl.BlockSpec(wq.shape, fixed2), keys, keys, keys, keys],
        out_specs=[pl.BlockSpec((ROUTE_TILE, D_MODEL), row), pl.BlockSpec((ROUTE_TILE, D_MODEL), row),
                   pl.BlockSpec((ROUTE_TILE, SLOTS), row), pl.BlockSpec((ROUTE_TILE, SLOTS), row)],
        out_shape=[jax.ShapeDtypeStruct((T, D_MODEL), F32), jax.ShapeDtypeStruct((T, D_MODEL), F32),
                   jax.ShapeDtypeStruct((T, SLOTS), jnp.int32), jax.ShapeDtypeStruct((T, SLOTS), F32)],
        compiler_params=pltpu.CompilerParams(vmem_limit_bytes=VMEM_LIMIT, dimension_semantics=("arbitrary",)),
        name="route",
    )(x2, sbo, swo, gsb, gsw, wout, gffn, wq, k1h, k1l, k2h, k2l)


def _pack_table(tab):
    b = lax.bitcast_convert_type(tab.astype(BF16), jnp.uint16).astype(jnp.uint32)
    return (b[:, :ROW_WORDS] | (b[:, ROW_WORDS:] << 16)).reshape(N_EXPERTS, ROW_SUB, LANES)


def _gather_rows(idx_ref, base, tab_ref, g_ref):
    for k in range(SLOTS):
        g_ref[pl.ds(k, ROW_SUB, stride=GSTRIDE), :] = tab_ref[idx_ref[base + k]]


def _unpack_chunk(g_ref, j):
    words = g_ref[pl.ds(j * GSTRIDE, SLOTS), :]
    lo = lax.bitcast_convert_type(words << 16, F32).astype(BF16)
    hi = lax.bitcast_convert_type(words & jnp.uint32(0xFFFF0000), F32).astype(BF16)
    return lo, hi


def _down_kernel(idx_ref, h2_ref, gate_ref, tab_ref, w_ref, g0, g1):
    gbufs = (g0, g1)

    def body(i, carry):
        t0 = pl.multiple_of(i * TOK_GROUP, TOK_GROUP)
        x_hi, x_lo = _split_bf16(h2_ref[pl.ds(t0, TOK_GROUP), :])
        rows = []
        for u in range(TOK_GROUP):
            g = gbufs[u % 2]
            _gather_rows(idx_ref, (t0 + u) * SLOTS, tab_ref, g)
            acc = jnp.zeros((2, SLOTS), F32)
            for j in range(ROW_SUB):
                lo, hi = _unpack_chunk(g, j)
                for part, c0 in ((lo, j * LANES), (hi, ROW_WORDS + j * LANES)):
                    lhs = jnp.concatenate([x_hi[u:u + 1, c0:c0 + LANES], x_lo[u:u + 1, c0:c0 + LANES]], axis=0)
                    acc = acc + _dot_t(lhs, part)
            rows.append(acc[0:1] + acc[1:2])
        hid = jnp.concatenate(rows, axis=0)
        gelu = 0.5 * hid * (1.0 + lax.erf(hid * (2.0 ** -0.5)))
        w_ref[pl.ds(t0, TOK_GROUP), :] = gate_ref[pl.ds(t0, TOK_GROUP), :] * gelu
        return carry

    lax.fori_loop(0, TOK_TILE // TOK_GROUP, body, 0)


def _up_kernel(idx_ref, w_ref, x1_ref, gfin_ref, tab_ref, y_ref, g0, g1):
    gbufs = (g0, g1)

    def body(i, carry):
        t0 = pl.multiple_of(i * TOK_GROUP, TOK_GROUP)
        w_hi, w_lo = _split_bf16(w_ref[pl.ds(t0, TOK_GROUP), :])
        rows = [[None] * TOK_GROUP for _ in range(2 * ROW_SUB)]
        for u in range(TOK_GROUP):
            g = gbufs[u % 2]
            _gather_rows(idx_ref, (t0 + u) * SLOTS, tab_ref, g)
            lhs = jnp.concatenate([w_hi[u:u + 1], w_lo[u:u + 1]], axis=0)
            for j in range(ROW_SUB):
                lo, hi = _unpack_chunk(g, j)
                r_lo = _dot(lhs, lo)
                r_hi = _dot(lhs, hi)
                rows[j][u] = r_lo[0:1] + r_lo[1:2]
                rows[ROW_SUB + j][u] = r_hi[0:1] + r_hi[1:2]
        peer = jnp.concatenate([jnp.concatenate(r, axis=0) for r in rows], axis=1)
        x2 = x1_ref[pl.ds(t0, TOK_GROUP), :] + peer
        y_ref[pl.ds(t0, TOK_GROUP), :] = _rms(x2, gfin_ref[...])
        return carry

    lax.fori_loop(0, TOK_TILE // TOK_GROUP, body, 0)


def _gather_specs():
    idx_spec = pl.BlockSpec((TOK_TILE * SLOTS,), lambda i: (i,), memory_space=pltpu.SMEM)
    tab_spec = pl.BlockSpec((N_EXPERTS, ROW_SUB, LANES), lambda i: (0, 0, 0), pipeline_mode=pl.Buffered(1))
    scratch = [pltpu.VMEM((ROW_SUB * GSTRIDE, LANES), jnp.uint32)] * 2
    params = pltpu.CompilerParams(vmem_limit_bytes=VMEM_LIMIT, dimension_semantics=("arbitrary",))
    return idx_spec, tab_spec, scratch, params


def _peer_down(idx_flat, h2, gate, tab):
    T = h2.shape[0]
    row = lambda i: (i, 0)
    idx_spec, tab_spec, scratch, params = _gather_specs()
    return pl.pallas_call(
        _down_kernel,
        grid=(T // TOK_TILE,),
        in_specs=[idx_spec, pl.BlockSpec((TOK_TILE, D_MODEL), row), pl.BlockSpec((TOK_TILE, SLOTS), row), tab_spec],
        out_specs=pl.BlockSpec((TOK_TILE, SLOTS), row),
        out_shape=jax.ShapeDtypeStruct((T, SLOTS), F32),
        scratch_shapes=scratch, compiler_params=params, name="peer_down",
    )(idx_flat, h2, gate, tab)


def _peer_up(idx_flat, w, x1, gfin, tab):
    T = x1.shape[0]
    row = lambda i: (i, 0)
    idx_spec, tab_spec, scratch, params = _gather_specs()
    return pl.pallas_call(
        _up_kernel,
        grid=(T // TOK_TILE,),
        in_specs=[idx_spec, pl.BlockSpec((TOK_TILE, SLOTS), row), pl.BlockSpec((TOK_TILE, D_MODEL), row),
                  pl.BlockSpec((1, D_MODEL), lambda i: (0, 0)), tab_spec],
        out_specs=pl.BlockSpec((TOK_TILE, D_MODEL), row),
        out_shape=jax.ShapeDtypeStruct((T, D_MODEL), F32),
        scratch_shapes=scratch, compiler_params=params, name="peer_up",
    )(idx_flat, w, x1, gfin, tab)


def _rope_table():
    half = HEAD_DIM // 2
    inv_freq = ROPE_THETA ** (-jnp.arange(half, dtype=F32) / half)
    lane = jnp.arange(LANES)
    sign = jnp.where(lane % HEAD_DIM < half, -1.0, 1.0).astype(F32)
    return jnp.stack([inv_freq[lane % half], sign])


def _extend_w_in(w):
    c0 = 3 * SB_WIDTH + SWA_Q_WIDTH
    kv = [w[:, c0 + n * HEAD_DIM:c0 + (n + 1) * HEAD_DIM] for n in range(2 * SWA_KV_HEADS)]
    return jnp.concatenate([w[:, :c0]] + [kv[n] for n in (0, 0, 1, 1, 2, 2, 3, 3)], axis=1).astype(BF16)


def kernel(x, positions, attn_norm, w_in, sb_out_norm, swa_sinks, swa_out_norm, w_out, ffn_norm, peer_w_query, peer_sub_keys_1, peer_sub_keys_2, peer_expert_down, peer_expert_up, final_norm):
    B, S, D = x.shape
    T = B * S
    assert w_in.shape[0] == 1, "single-layer trunk: the final norm is fused into the last kernel"
    x2 = x.reshape(T, D)
    sbq, sbk, sbv, swq, swk, swv = _proj(x2, positions.reshape(T, 1), attn_norm[0].reshape(1, D),
                                         _extend_w_in(w_in[0]), _rope_table())
    sbo = _sb_attention(sbq.reshape(B, S, -1), sbk.reshape(B, S, -1), sbv.reshape(B, S, -1))
    swo = _swa_attention(swa_sinks[0].reshape(1, -1), swq.reshape(B, S, -1), swk.reshape(B, S, -1),
                         swv.reshape(B, S, -1))
    k1h, k1l = _split_bf16(peer_sub_keys_1[0])
    k2h, k2l = _split_bf16(peer_sub_keys_2[0])
    x1, h2, idx, gate = _route(x2, sbo.reshape(T, -1), swo.reshape(T, -1), sb_out_norm[0].reshape(1, -1),
                               swa_out_norm[0].reshape(1, -1), w_out[0].astype(BF16), ffn_norm[0].reshape(1, D),
                               peer_w_query[0].astype(BF16), k1h, k1l, k2h, k2l)
    idx_flat = idx.reshape(T * SLOTS)
    w = _peer_down(idx_flat, h2, gate, _pack_table(peer_expert_down[0]))
    y = _peer_up(idx_flat, w, x1, final_norm.reshape(1, D), _pack_table(peer_expert_up[0]))
    return y.reshape(B, S, D)
```

```python
import functools
import math

import jax
import jax.numpy as jnp
from jax import lax
from jax.experimental import pallas as pl
from jax.experimental.pallas import tpu as pltpu

D_MODEL = 1024
HEAD_DIM = 64
SB_WIDTH = 512
SWA_Q_WIDTH = 512
SWA_KV_HEADS = 2
SWA_HEADS = 8
BLOCK = 128
ROPE_THETA = 10000.0
N_KEYS = 128
N_EXPERTS = N_KEYS * N_KEYS
PEER_HEADS = 8
PEER_HALF = 128
PEER_TOPK = 16
SLOTS = PEER_HEADS * PEER_TOPK
EPS = 1e-6
ATTN_SCALE = HEAD_DIM ** -0.5

LANES = 128
SUBLANES = 8
VMEM_LIMIT = 56 * 1024 * 1024

PROJ_TILE = 512
ROUTE_TILE = 256
TOK_TILE = 128
TOK_GROUP = SUBLANES
ROW_CHUNKS = D_MODEL // LANES
GROWS = SLOTS * ROW_CHUNKS

SB_EXIT = 104.0

F32 = jnp.float32
BF16 = jnp.bfloat16


def _rms(v, g):
    return v * lax.rsqrt(jnp.mean(v * v, axis=-1, keepdims=True) + EPS) * g


def _split_bf16(v):
    hi = v.astype(BF16)
    return hi, (v - hi.astype(F32)).astype(BF16)


def _dot_t(a, b):
    return lax.dot_general(a, b, (((1,), (1,)), ((), ())), preferred_element_type=F32)


def _dot(a, b):
    return jnp.dot(a, b, preferred_element_type=F32)


def _proj_kernel(x_ref, pos_ref, g_ref, w_ref, rope_ref, sbq_ref, sbk_ref, sbv_ref, swq_ref, swk_ref, swv_ref):
    h = _rms(x_ref[...], g_ref[...]).astype(BF16)
    ang = pos_ref[...].astype(F32) * rope_ref[0:1, :]
    sign = rope_ref[1:2, :]
    cos = jnp.cos(ang)
    sin = jnp.sin(ang) * sign
    first_half = sign < 0

    def rope(v):
        partner = jnp.where(first_half, pltpu.roll(v, LANES - HEAD_DIM // 2, 1), pltpu.roll(v, HEAD_DIM // 2, 1))
        return v * cos + partner * sin

    def proj(c0, n):
        return _dot(h, w_ref[:, c0:c0 + n])

    sbq_ref[...] = (proj(0, SB_WIDTH) * ATTN_SCALE).astype(BF16)
    sbk_ref[...] = proj(SB_WIDTH, SB_WIDTH).astype(BF16)
    sbv_ref[...] = proj(2 * SB_WIDTH, SB_WIDTH).astype(BF16)
    c0 = 3 * SB_WIDTH
    for c in range(SWA_Q_WIDTH // LANES):
        swq_ref[:, c * LANES:(c + 1) * LANES] = (rope(proj(c0 + c * LANES, LANES)) * ATTN_SCALE).astype(BF16)
    c0 += SWA_Q_WIDTH
    for g in range(SWA_KV_HEADS):
        swk_ref[:, g * LANES:(g + 1) * LANES] = rope(proj(c0 + g * LANES, LANES)).astype(BF16)
    c0 += SWA_KV_HEADS * LANES
    swv_ref[...] = proj(c0, SWA_KV_HEADS * LANES).astype(BF16)


def _proj(x2, pos2, g, w_ext, rope_tab):
    T = x2.shape[0]
    row = lambda i: (i, 0)
    fixed = lambda i: (0, 0)
    kvw = SWA_KV_HEADS * LANES
    return pl.pallas_call(
        _proj_kernel,
        grid=(T // PROJ_TILE,),
        in_specs=[pl.BlockSpec((PROJ_TILE, D_MODEL), row), pl.BlockSpec((PROJ_TILE, 1), row),
                  pl.BlockSpec((1, D_MODEL), fixed), pl.BlockSpec(w_ext.shape, fixed),
                  pl.BlockSpec(rope_tab.shape, fixed)],
        out_specs=[pl.BlockSpec((PROJ_TILE, SB_WIDTH), row)] * 3
        + [pl.BlockSpec((PROJ_TILE, SWA_Q_WIDTH), row), pl.BlockSpec((PROJ_TILE, kvw), row),
           pl.BlockSpec((PROJ_TILE, kvw), row)],
        out_shape=[jax.ShapeDtypeStruct((T, SB_WIDTH), BF16)] * 3
        + [jax.ShapeDtypeStruct((T, SWA_Q_WIDTH), BF16), jax.ShapeDtypeStruct((T, kvw), BF16),
           jax.ShapeDtypeStruct((T, kvw), BF16)],
        compiler_params=pltpu.CompilerParams(vmem_limit_bytes=VMEM_LIMIT, dimension_semantics=("arbitrary",)),
        name="proj",
    )(x2, pos2, g, w_ext, rope_tab)


def _sb_kernel(q_ref, k_ref, v_ref, o_ref):
    i = pl.program_id(2)
    q = q_ref[...]
    lane = lax.broadcasted_iota(jnp.int32, (BLOCK, LANES), 1)
    row = lax.broadcasted_iota(jnp.int32, (BLOCK, BLOCK), 0)
    col = lax.broadcasted_iota(jnp.int32, (BLOCK, BLOCK), 1)
    causal = col < row
    rj = lax.broadcasted_iota(jnp.int32, (BLOCK, 2 * BLOCK), 0)
    cj = lax.broadcasted_iota(jnp.int32, (BLOCK, 2 * BLOCK), 1)
    suffix = jnp.where((cj >= BLOCK) | (rj > cj), 1.0, 0.0).astype(BF16)

    def tile(qh, j, carry, diagonal):
        start = pl.multiple_of(j * BLOCK, BLOCK)
        k = k_ref[pl.ds(start, BLOCK), :]
        v = v_ref[pl.ds(start, BLOCK), :]
        z = _dot_t(qh, k)
        sp = jnp.maximum(z, 0.0) + jnp.log1p(jnp.exp(-jnp.abs(z)))
        lom = jnp.where(causal, -sp, 0.0) if diagonal else -sp
        hi, lo = _split_bf16(lom)
        sums = _dot(hi, suffix) + _dot(lo, suffix)
        w = jnp.exp(z - sp + sums[:, :BLOCK] + carry)
        if diagonal:
            w = jnp.where(causal, w, 0.0)
        return _dot(w.astype(BF16), v), carry + sums[:, BLOCK:]

    outs = []
    for h in range(2):
        qh = jnp.where((lane >= HEAD_DIM) == (h == 1), q, jnp.zeros_like(q))
        acc, carry = tile(qh, i, jnp.zeros((BLOCK, BLOCK), F32), True)

        def cond(s):
            return jnp.logical_and(s[0] >= 0, jnp.max(s[2]) > -SB_EXIT)

        def body(s, qh=qh):
            pv, carry = tile(qh, s[0], s[2], False)
            return s[0] - 1, s[1] + pv, carry

        _, acc, _ = lax.while_loop(cond, body, (i - 1, acc, carry))
        outs.append(acc)
    o_ref[...] = jnp.where(lane < HEAD_DIM, outs[0], outs[1])


def _sb_attention(q, k, v):
    B, S, _ = q.shape
    qmap = lambda b, p, i: (b, i, p)
    kvmap = lambda b, p, i: (b, 0, p)
    return pl.pallas_call(
        _sb_kernel,
        grid=(B, SB_WIDTH // LANES, S // BLOCK),
        in_specs=[pl.BlockSpec((None, BLOCK, LANES), qmap), pl.BlockSpec((None, S, LANES), kvmap),
                  pl.BlockSpec((None, S, LANES), kvmap)],
        out_specs=pl.BlockSpec((None, BLOCK, LANES), qmap),
        out_shape=jax.ShapeDtypeStruct((B, S, SB_WIDTH), F32),
        compiler_params=pltpu.CompilerParams(vmem_limit_bytes=VMEM_LIMIT,
                                             dimension_semantics=("arbitrary", "arbitrary", "arbitrary")),
        name="sb_attention",
    )(q, k, v)


def _swa_kernel(sink_ref, q_ref, kp_ref, kc_ref, vp_ref, vc_ref, o_ref):
    i = pl.program_id(1)
    r = lax.broadcasted_iota(jnp.int32, (BLOCK, 2 * BLOCK), 0)
    c = lax.broadcasted_iota(jnp.int32, (BLOCK, 2 * BLOCK), 1)
    valid = (c > r) & (c <= r + BLOCK) & ((c >= BLOCK) | (i > 0))
    lane = lax.broadcasted_iota(jnp.int32, (BLOCK, LANES), 1)
    heads_per_kv = SWA_HEADS // SWA_KV_HEADS
    for ch in range(SWA_Q_WIDTH // LANES):
        g = (2 * ch) // heads_per_kv
        sl = slice(g * LANES, (g + 1) * LANES)
        q = q_ref[:, ch * LANES:(ch + 1) * LANES]
        k = jnp.concatenate([kp_ref[:, sl], kc_ref[:, sl]], axis=0)
        v = jnp.concatenate([vp_ref[:, sl], vc_ref[:, sl]], axis=0)
        outs = []
        for hh in range(2):
            qh = jnp.where((lane >= HEAD_DIM) == (hh == 1), q, jnp.zeros_like(q))
            z = jnp.where(valid, _dot_t(qh, k), -1e30)
            sink = sink_ref[0, 2 * ch + hh]
            m = jnp.maximum(jnp.max(z, axis=1, keepdims=True), sink)
            e = jnp.exp(z - m)
            denom = jnp.sum(e, axis=1, keepdims=True) + jnp.exp(sink - m)
            outs.append(_dot((e / denom).astype(BF16), v))
        o_ref[:, ch * LANES:(ch + 1) * LANES] = jnp.where(lane < HEAD_DIM, outs[0], outs[1])


def _swa_attention(sinks, q, k, v):
    B, S, _ = q.shape
    kvw = SWA_KV_HEADS * LANES
    cur = lambda b, i: (b, i, 0)
    prev = lambda b, i: (b, jnp.maximum(i - 1, 0), 0)
    return pl.pallas_call(
        _swa_kernel,
        grid=(B, S // BLOCK),
        in_specs=[pl.BlockSpec((1, SWA_HEADS), lambda b, i: (0, 0), memory_space=pltpu.SMEM),
                  pl.BlockSpec((None, BLOCK, SWA_Q_WIDTH), cur),
                  pl.BlockSpec((None, BLOCK, kvw), prev), pl.BlockSpec((None, BLOCK, kvw), cur),
                  pl.BlockSpec((None, BLOCK, kvw), prev), pl.BlockSpec((None, BLOCK, kvw), cur)],
        out_specs=pl.BlockSpec((None, BLOCK, SWA_Q_WIDTH), cur),
        out_shape=jax.ShapeDtypeStruct((B, S, SWA_Q_WIDTH), F32),
        compiler_params=pltpu.CompilerParams(vmem_limit_bytes=VMEM_LIMIT,
                                             dimension_semantics=("arbitrary", "arbitrary")),
        name="swa_attention",
    )(sinks, q, k, k, v, v)


def _topk_rows(s, n, payload=None):
    rows = s.shape[0]
    iota = lax.broadcasted_iota(jnp.int32, s.shape, 0)
    vals, picks = [], []
    for _ in range(n):
        m = jnp.max(s, axis=0, keepdims=True)
        pos = jnp.min(jnp.where(s == m, iota, rows), axis=0, keepdims=True)
        hit = iota == pos
        vals.append(m)
        picks.append(pos if payload is None else jnp.max(jnp.where(hit, payload, -1), axis=0, keepdims=True))
        s = jnp.where(hit, -jnp.inf, s)
    return jnp.concatenate(vals, axis=0), jnp.concatenate(picks, axis=0)


def _route_kernel(x_ref, sbo_ref, swo_ref, gsb_ref, gsw_ref, wout_ref, gffn_ref, wq_ref,
                  k1h_ref, k1l_ref, k2h_ref, k2l_ref, x1_ref, h2_ref, idx_ref, gate_ref):
    sbn = _rms(sbo_ref[...], gsb_ref[...]).astype(BF16)
    swn = _rms(swo_ref[...], gsw_ref[...]).astype(BF16)
    x1 = x_ref[...] + _dot(sbn, wout_ref[0:SB_WIDTH, :]) + _dot(swn, wout_ref[SB_WIDTH:, :])
    x1_ref[...] = x1
    h2 = _rms(x1, gffn_ref[...])
    h2_ref[...] = h2
    q = _dot(h2.astype(BF16), wq_ref[...])

    def scores(qq, kh_ref, kl_ref, h):
        qh, ql = _split_bf16(qq)
        return _dot_t(kh_ref[h], qh) + _dot_t(kh_ref[h], ql) + _dot_t(kl_ref[h], qh)

    idx_rows, gate_rows = [], []
    for h in range(PEER_HEADS):
        c0 = h * 2 * PEER_HALF
        v1, i1 = _topk_rows(scores(q[:, c0:c0 + PEER_HALF], k1h_ref, k1l_ref, h), PEER_TOPK)
        v2, i2 = _topk_rows(scores(q[:, c0 + PEER_HALF:c0 + 2 * PEER_HALF], k2h_ref, k2l_ref, h), PEER_TOPK)
        cand = jnp.concatenate([v1[a:a + 1] + v2 for a in range(PEER_TOPK)], axis=0)
        cidx = jnp.concatenate([i1[a:a + 1] * N_KEYS + i2 for a in range(PEER_TOPK)], axis=0)
        top_s, idx = _topk_rows(cand, PEER_TOPK, payload=cidx)
        e = jnp.exp(top_s - top_s[0:1])
        idx_rows.append(idx)
        gate_rows.append(e / jnp.sum(e, axis=0, keepdims=True))
    idx_t = jnp.concatenate(idx_rows, axis=0).astype(F32)
    idx_ref[...] = idx_t.T.astype(jnp.int32)
    gate_ref[...] = jnp.concatenate(gate_rows, axis=0).T


def _route(x2, sbo, swo, gsb, gsw, wout, gffn, wq, k1h, k1l, k2h, k2l):
    T = x2.shape[0]
    row = lambda i: (i, 0)
    fixed2 = lambda i: (0, 0)
    fixed3 = lambda i: (0, 0, 0)
    keys = pl.BlockSpec(k1h.shape, fixed3)
    return pl.pallas_call(
        _route_kernel,
        grid=(T // ROUTE_TILE,),
        in_specs=[pl.BlockSpec((ROUTE_TILE, D_MODEL), row), pl.BlockSpec((ROUTE_TILE, SB_WIDTH), row),
                  pl.BlockSpec((ROUTE_TILE, SWA_Q_WIDTH), row),
                  pl.BlockSpec((1, SB_WIDTH), fixed2), pl.BlockSpec((1, SWA_Q_WIDTH), fixed2),
                  pl.BlockSpec(wout.shape, fixed2), pl.BlockSpec((1, D_MODEL), fixed2),
                  pl.BlockSpec(wq.shape, fixed2), keys, keys, keys, keys],
        out_specs=[pl.BlockSpec((ROUTE_TILE, D_MODEL), row), pl.BlockSpec((ROUTE_TILE, D_MODEL), row),
                   pl.BlockSpec((ROUTE_TILE, SLOTS), row), pl.BlockSpec((ROUTE_TILE, SLOTS), row)],
        out_shape=[jax.ShapeDtypeStruct((T, D_MODEL), F32), jax.ShapeDtypeStruct((T, D_MODEL), F32),
                   jax.ShapeDtypeStruct((T, SLOTS), jnp.int32), jax.ShapeDtypeStruct((T, SLOTS), F32)],
        compiler_params=pltpu.CompilerParams(vmem_limit_bytes=VMEM_LIMIT, dimension_semantics=("arbitrary",)),
        name="route",
    )(x2, sbo, swo, gsb, gsw, wout, gffn, wq, k1h, k1l, k2h, k2l)


def _tile_table(tab):
    return tab.astype(BF16).reshape(N_EXPERTS, ROW_CHUNKS, LANES)


def _gather_tiles(idx_ref, base, tab_ref, g_ref):
    for k in range(SLOTS):
        g_ref[pl.ds(k * ROW_CHUNKS, ROW_CHUNKS), :] = tab_ref[idx_ref[base + k]]


def _block_diag_mask():
    c = lax.broadcasted_iota(jnp.int32, (ROW_CHUNKS, GROWS), 0)
    j = lax.broadcasted_iota(jnp.int32, (ROW_CHUNKS, GROWS), 1)
    return (j % ROW_CHUNKS) == c


def _split_rows(v):
    hi, lo = _split_bf16(v)
    return jnp.concatenate([hi, lo], axis=0)


def _fold_rows(v):
    m = v.shape[0] // 2
    return v[:m] + v[m:]


def _down_kernel(idx_ref, h2t_ref, gate_ref, tab_ref, w_ref, g0, g1):
    gbufs = (g0, g1)
    diag = _block_diag_mask()
    j = lax.broadcasted_iota(jnp.int32, (GROWS, SLOTS), 0)
    k = lax.broadcasted_iota(jnp.int32, (GROWS, SLOTS), 1)
    pool = jnp.where(j // ROW_CHUNKS == k, 1.0, 0.0).astype(BF16)

    def body(i, carry):
        t0 = pl.multiple_of(i * TOK_GROUP, TOK_GROUP)
        rows = []
        for u in range(TOK_GROUP):
            g = gbufs[u % 2]
            _gather_tiles(idx_ref, (t0 + u) * SLOTS, tab_ref, g)
            z = _fold_rows(_dot_t(_split_rows(h2t_ref[t0 + u]), g[...]))
            rows.append(jnp.sum(jnp.where(diag, z, 0.0), axis=0, keepdims=True))
        hid = _fold_rows(_dot(_split_rows(jnp.concatenate(rows, axis=0)), pool))
        gelu = 0.5 * hid * (1.0 + lax.erf(hid * (2.0 ** -0.5)))
        w_ref[pl.ds(t0, TOK_GROUP), :] = gate_ref[pl.ds(t0, TOK_GROUP), :] * gelu
        return carry

    lax.fori_loop(0, TOK_TILE // TOK_GROUP, body, 0)


def _up_kernel(idx_ref, w_ref, x1t_ref, gfin_ref, tab_ref, y_ref, g0, g1):
    gbufs = (g0, g1)
    diag = _block_diag_mask()
    k = lax.broadcasted_iota(jnp.int32, (SLOTS, GROWS), 0)
    j = lax.broadcasted_iota(jnp.int32, (SLOTS, GROWS), 1)
    spread = jnp.where(j // ROW_CHUNKS == k, 1.0, 0.0).astype(BF16)

    def body(i, carry):
        t0 = pl.multiple_of(i * TOK_GROUP, TOK_GROUP)
        wrep = _fold_rows(_dot(_split_rows(w_ref[pl.ds(t0, TOK_GROUP), :]), spread))
        for u in range(TOK_GROUP):
            g = gbufs[u % 2]
            _gather_tiles(idx_ref, (t0 + u) * SLOTS, tab_ref, g)
            wexp = jnp.where(diag, jnp.broadcast_to(wrep[u:u + 1], (ROW_CHUNKS, GROWS)), 0.0)
            x2 = x1t_ref[t0 + u] + _fold_rows(_dot(_split_rows(wexp), g[...]))
            ms = jnp.sum(jnp.sum(x2 * x2, axis=1, keepdims=True), axis=0, keepdims=True) * (1.0 / D_MODEL)
            y_ref[t0 + u] = x2 * lax.rsqrt(ms + EPS) * gfin_ref[...]
        return carry

    lax.fori_loop(0, TOK_TILE // TOK_GROUP, body, 0)


def _gather_specs():
    idx_spec = pl.BlockSpec((TOK_TILE * SLOTS,), lambda i: (i,), memory_space=pltpu.SMEM)
    tab_spec = pl.BlockSpec((N_EXPERTS, ROW_CHUNKS, LANES), lambda i: (0, 0, 0), pipeline_mode=pl.Buffered(1))
    scratch = [pltpu.VMEM((GROWS, LANES), BF16)] * 2
    params = pltpu.CompilerParams(vmem_limit_bytes=VMEM_LIMIT, dimension_semantics=("arbitrary",))
    return idx_spec, tab_spec, scratch, params


def _peer_down(idx_flat, h2t, gate, tab):
    T = h2t.shape[0]
    row = lambda i: (i, 0)
    tok = pl.BlockSpec((TOK_TILE, ROW_CHUNKS, LANES), lambda i: (i, 0, 0))
    idx_spec, tab_spec, scratch, params = _gather_specs()
    return pl.pallas_call(
        _down_kernel,
        grid=(T // TOK_TILE,),
        in_specs=[idx_spec, tok, pl.BlockSpec((TOK_TILE, SLOTS), row), tab_spec],
        out_specs=pl.BlockSpec((TOK_TILE, SLOTS), row),
        out_shape=jax.ShapeDtypeStruct((T, SLOTS), F32),
        scratch_shapes=scratch, compiler_params=params, name="peer_down",
    )(idx_flat, h2t, gate, tab)


def _peer_up(idx_flat, w, x1t, gfin_t, tab):
    T = x1t.shape[0]
    tok = pl.BlockSpec((TOK_TILE, ROW_CHUNKS, LANES), lambda i: (i, 0, 0))
    idx_spec, tab_spec, scratch, params = _gather_specs()
    return pl.pallas_call(
        _up_kernel,
        grid=(T // TOK_TILE,),
        in_specs=[idx_spec, pl.BlockSpec((TOK_TILE, SLOTS), lambda i: (i, 0)), tok,
                  pl.BlockSpec((ROW_CHUNKS, LANES), lambda i: (0, 0)), tab_spec],
        out_specs=tok,
        out_shape=jax.ShapeDtypeStruct((T, ROW_CHUNKS, LANES), F32),
        scratch_shapes=scratch, compiler_params=params, name="peer_up",
    )(idx_flat, w, x1t, gfin_t, tab)


def _rope_table():
    half = HEAD_DIM // 2
    inv_freq = ROPE_THETA ** (-jnp.arange(half, dtype=F32) / half)
    lane = jnp.arange(LANES)
    sign = jnp.where(lane % HEAD_DIM < half, -1.0, 1.0).astype(F32)
    return jnp.stack([inv_freq[lane % half], sign])


def _extend_w_in(w):
    c0 = 3 * SB_WIDTH + SWA_Q_WIDTH
    kv = [w[:, c0 + n * HEAD_DIM:c0 + (n + 1) * HEAD_DIM] for n in range(2 * SWA_KV_HEADS)]
    return jnp.concatenate([w[:, :c0]] + [kv[n] for n in (0, 0, 1, 1, 2, 2, 3, 3)], axis=1).astype(BF16)


def kernel(x, positions, attn_norm, w_in, sb_out_norm, swa_sinks, swa_out_norm, w_out, ffn_norm, peer_w_query, peer_sub_keys_1, peer_sub_keys_2, peer_expert_down, peer_expert_up, final_norm):
    B, S, D = x.shape
    T = B * S
    assert w_in.shape[0] == 1, "single-layer trunk: the final norm is fused into the last kernel"
    x2 = x.reshape(T, D)
    sbq, sbk, sbv, swq, swk, swv = _proj(x2, positions.reshape(T, 1), attn_norm[0].reshape(1, D),
                                         _extend_w_in(w_in[0]), _rope_table())
    sbo = _sb_attention(sbq.reshape(B, S, -1), sbk.reshape(B, S, -1), sbv.reshape(B, S, -1))
    swo = _swa_attention(swa_sinks[0].reshape(1, -1), swq.reshape(B, S, -1), swk.reshape(B, S, -1),
                         swv.reshape(B, S, -1))
    k1h, k1l = _split_bf16(peer_sub_keys_1[0])
    k2h, k2l = _split_bf16(peer_sub_keys_2[0])
    x1, h2, idx, gate = _route(x2, sbo.reshape(T, -1), swo.reshape(T, -1), sb_out_norm[0].reshape(1, -1),
                               swa_out_norm[0].reshape(1, -1), w_out[0].astype(BF16), ffn_norm[0].reshape(1, D),
                               peer_w_query[0].astype(BF16), k1h, k1l, k2h, k2l)
    idx_flat = idx.reshape(T * SLOTS)
    tiles = (T, ROW_CHUNKS, LANES)
    w = _peer_down(idx_flat, h2.reshape(tiles), gate, _tile_table(peer_expert_down[0]))
    y = _peer_up(idx_flat, w, x1.reshape(tiles), final_norm.reshape(ROW_CHUNKS, LANES),
                 _tile_table(peer_expert_up[0]))
    return y.reshape(B, S, D)
```

```python
import functools
import math

import jax
import jax.numpy as jnp
from jax import lax
from jax.experimental import pallas as pl
from jax.experimental.pallas import tpu as pltpu

D_MODEL = 1024
HEAD_DIM = 64
SB_WIDTH = 512
SWA_Q_WIDTH = 512
SWA_KV_HEADS = 2
SWA_HEADS = 8
BLOCK = 128
ROPE_THETA = 10000.0
N_KEYS = 128
N_EXPERTS = N_KEYS * N_KEYS
PEER_HEADS = 8
PEER_HALF = 128
PEER_TOPK = 16
SLOTS = PEER_HEADS * PEER_TOPK
EPS = 1e-6
ATTN_SCALE = HEAD_DIM ** -0.5

LANES = 128
SUBLANES = 8
VMEM_LIMIT = 56 * 1024 * 1024

PROJ_TILE = 512
ROUTE_TILE = 256
TOK_TILE = 128
TOK_GROUP = SUBLANES
ROW_CHUNKS = D_MODEL // LANES
GROWS = SLOTS * ROW_CHUNKS
GROUP_IDX = TOK_GROUP * SLOTS
GROUPS_PER_STEP = TOK_TILE // TOK_GROUP

SB_EXIT = 104.0

F32 = jnp.float32
BF16 = jnp.bfloat16


def _rms(v, g):
    return v * lax.rsqrt(jnp.mean(v * v, axis=-1, keepdims=True) + EPS) * g


def _split_bf16(v):
    hi = v.astype(BF16)
    return hi, (v - hi.astype(F32)).astype(BF16)


def _dot_t(a, b):
    return lax.dot_general(a, b, (((1,), (1,)), ((), ())), preferred_element_type=F32)


def _dot(a, b):
    return jnp.dot(a, b, preferred_element_type=F32)


def _proj_kernel(x_ref, pos_ref, g_ref, w_ref, rope_ref, sbq_ref, sbk_ref, sbv_ref, swq_ref, swk_ref, swv_ref):
    h = _rms(x_ref[...], g_ref[...]).astype(BF16)
    ang = pos_ref[...].astype(F32) * rope_ref[0:1, :]
    sign = rope_ref[1:2, :]
    cos = jnp.cos(ang)
    sin = jnp.sin(ang) * sign
    first_half = sign < 0

    def rope(v):
        partner = jnp.where(first_half, pltpu.roll(v, LANES - HEAD_DIM // 2, 1), pltpu.roll(v, HEAD_DIM // 2, 1))
        return v * cos + partner * sin

    def proj(c0, n):
        return _dot(h, w_ref[:, c0:c0 + n])

    sbq_ref[...] = (proj(0, SB_WIDTH) * ATTN_SCALE).astype(BF16)
    sbk_ref[...] = proj(SB_WIDTH, SB_WIDTH).astype(BF16)
    sbv_ref[...] = proj(2 * SB_WIDTH, SB_WIDTH).astype(BF16)
    c0 = 3 * SB_WIDTH
    for c in range(SWA_Q_WIDTH // LANES):
        swq_ref[:, c * LANES:(c + 1) * LANES] = (rope(proj(c0 + c * LANES, LANES)) * ATTN_SCALE).astype(BF16)
    c0 += SWA_Q_WIDTH
    for g in range(SWA_KV_HEADS):
        swk_ref[:, g * LANES:(g + 1) * LANES] = rope(proj(c0 + g * LANES, LANES)).astype(BF16)
    c0 += SWA_KV_HEADS * LANES
    swv_ref[...] = proj(c0, SWA_KV_HEADS * LANES).astype(BF16)


def _proj(x2, pos2, g, w_ext, rope_tab):
    T = x2.shape[0]
    row = lambda i: (i, 0)
    fixed = lambda i: (0, 0)
    kvw = SWA_KV_HEADS * LANES
    return pl.pallas_call(
        _proj_kernel,
        grid=(T // PROJ_TILE,),
        in_specs=[pl.BlockSpec((PROJ_TILE, D_MODEL), row), pl.BlockSpec((PROJ_TILE, 1), row),
                  pl.BlockSpec((1, D_MODEL), fixed), pl.BlockSpec(w_ext.shape, fixed),
                  pl.BlockSpec(rope_tab.shape, fixed)],
        out_specs=[pl.BlockSpec((PROJ_TILE, SB_WIDTH), row)] * 3
        + [pl.BlockSpec((PROJ_TILE, SWA_Q_WIDTH), row), pl.BlockSpec((PROJ_TILE, kvw), row),
           pl.BlockSpec((PROJ_TILE, kvw), row)],
        out_shape=[jax.ShapeDtypeStruct((T, SB_WIDTH), BF16)] * 3
        + [jax.ShapeDtypeStruct((T, SWA_Q_WIDTH), BF16), jax.ShapeDtypeStruct((T, kvw), BF16),
           jax.ShapeDtypeStruct((T, kvw), BF16)],
        compiler_params=pltpu.CompilerParams(vmem_limit_bytes=VMEM_LIMIT, dimension_semantics=("arbitrary",)),
        name="proj",
    )(x2, pos2, g, w_ext, rope_tab)


def _sb_kernel(q_ref, k_ref, v_ref, o_ref):
    i = pl.program_id(2)
    q = q_ref[...]
    lane = lax.broadcasted_iota(jnp.int32, (BLOCK, LANES), 1)
    row = lax.broadcasted_iota(jnp.int32, (BLOCK, BLOCK), 0)
    col = lax.broadcasted_iota(jnp.int32, (BLOCK, BLOCK), 1)
    causal = col < row
    rj = lax.broadcasted_iota(jnp.int32, (BLOCK, 2 * BLOCK), 0)
    cj = lax.broadcasted_iota(jnp.int32, (BLOCK, 2 * BLOCK), 1)
    suffix = jnp.where((cj >= BLOCK) | (rj > cj), 1.0, 0.0).astype(BF16)

    def tile(qh, j, carry, diagonal):
        start = pl.multiple_of(j * BLOCK, BLOCK)
        k = k_ref[pl.ds(start, BLOCK), :]
        v = v_ref[pl.ds(start, BLOCK), :]
        z = _dot_t(qh, k)
        sp = jnp.maximum(z, 0.0) + jnp.log1p(jnp.exp(-jnp.abs(z)))
        lom = jnp.where(causal, -sp, 0.0) if diagonal else -sp
        hi, lo = _split_bf16(lom)
        sums = _dot(hi, suffix) + _dot(lo, suffix)
        w = jnp.exp(z - sp + sums[:, :BLOCK] + carry)
        if diagonal:
            w = jnp.where(causal, w, 0.0)
        return _dot(w.astype(BF16), v), carry + sums[:, BLOCK:]

    heads = [jnp.where((lane >= HEAD_DIM) == (h == 1), q, jnp.zeros_like(q)) for h in range(2)]
    zero = jnp.zeros((BLOCK, BLOCK), F32)
    first = [tile(qh, i, zero, True) for qh in heads]

    def cond(s):
        return jnp.logical_and(s[0] >= 0, jnp.max(jnp.maximum(s[3], s[4])) > -SB_EXIT)

    def body(s):
        j = s[0]
        has_second = j >= 1
        j2 = jnp.maximum(j - 1, 0)
        accs, carries = [], []
        for h in range(2):
            pv_a, carry = tile(heads[h], j, s[3 + h], False)
            pv_b, carry = tile(heads[h], j2, carry, False)
            accs.append(s[1 + h] + pv_a + jnp.where(has_second, pv_b, 0.0))
            carries.append(carry)
        return j - 2, accs[0], accs[1], carries[0], carries[1]

    out = lax.while_loop(cond, body, (i - 1, first[0][0], first[1][0], first[0][1], first[1][1]))
    o_ref[...] = jnp.where(lane < HEAD_DIM, out[1], out[2])


def _sb_attention(q, k, v):
    B, S, _ = q.shape
    qmap = lambda b, p, i: (b, i, p)
    kvmap = lambda b, p, i: (b, 0, p)
    return pl.pallas_call(
        _sb_kernel,
        grid=(B, SB_WIDTH // LANES, S // BLOCK),
        in_specs=[pl.BlockSpec((None, BLOCK, LANES), qmap), pl.BlockSpec((None, S, LANES), kvmap),
                  pl.BlockSpec((None, S, LANES), kvmap)],
        out_specs=pl.BlockSpec((None, BLOCK, LANES), qmap),
        out_shape=jax.ShapeDtypeStruct((B, S, SB_WIDTH), F32),
        compiler_params=pltpu.CompilerParams(vmem_limit_bytes=VMEM_LIMIT,
                                             dimension_semantics=("arbitrary", "arbitrary", "arbitrary")),
        name="sb_attention",
    )(q, k, v)


def _swa_kernel(sink_ref, q_ref, kp_ref, kc_ref, vp_ref, vc_ref, o_ref):
    i = pl.program_id(1)
    r = lax.broadcasted_iota(jnp.int32, (BLOCK, 2 * BLOCK), 0)
    c = lax.broadcasted_iota(jnp.int32, (BLOCK, 2 * BLOCK), 1)
    valid = (c > r) & (c <= r + BLOCK) & ((c >= BLOCK) | (i > 0))
    lane = lax.broadcasted_iota(jnp.int32, (BLOCK, LANES), 1)
    heads_per_kv = SWA_HEADS // SWA_KV_HEADS
    for ch in range(SWA_Q_WIDTH // LANES):
        g = (2 * ch) // heads_per_kv
        sl = slice(g * LANES, (g + 1) * LANES)
        q = q_ref[:, ch * LANES:(ch + 1) * LANES]
        k = jnp.concatenate([kp_ref[:, sl], kc_ref[:, sl]], axis=0)
        v = jnp.concatenate([vp_ref[:, sl], vc_ref[:, sl]], axis=0)
        outs = []
        for hh in range(2):
            qh = jnp.where((lane >= HEAD_DIM) == (hh == 1), q, jnp.zeros_like(q))
            z = jnp.where(valid, _dot_t(qh, k), -1e30)
            sink = sink_ref[0, 2 * ch + hh]
            m = jnp.maximum(jnp.max(z, axis=1, keepdims=True), sink)
            e = jnp.exp(z - m)
            denom = jnp.sum(e, axis=1, keepdims=True) + jnp.exp(sink - m)
            outs.append(_dot((e / denom).astype(BF16), v))
        o_ref[:, ch * LANES:(ch + 1) * LANES] = jnp.where(lane < HEAD_DIM, outs[0], outs[1])


def _swa_attention(sinks, q, k, v):
    B, S, _ = q.shape
    kvw = SWA_KV_HEADS * LANES
    cur = lambda b, i: (b, i, 0)
    prev = lambda b, i: (b, jnp.maximum(i - 1, 0), 0)
    return pl.pallas_call(
        _swa_kernel,
        grid=(B, S // BLOCK),
        in_specs=[pl.BlockSpec((1, SWA_HEADS), lambda b, i: (0, 0), memory_space=pltpu.SMEM),
                  pl.BlockSpec((None, BLOCK, SWA_Q_WIDTH), cur),
                  pl.BlockSpec((None, BLOCK, kvw), prev), pl.BlockSpec((None, BLOCK, kvw), cur),
                  pl.BlockSpec((None, BLOCK, kvw), prev), pl.BlockSpec((None, BLOCK, kvw), cur)],
        out_specs=pl.BlockSpec((None, BLOCK, SWA_Q_WIDTH), cur),
        out_shape=jax.ShapeDtypeStruct((B, S, SWA_Q_WIDTH), F32),
        compiler_params=pltpu.CompilerParams(vmem_limit_bytes=VMEM_LIMIT,
                                             dimension_semantics=("arbitrary", "arbitrary")),
        name="swa_attention",
    )(sinks, q, k, k, v, v)


def _topk_rows(s, n, payload=None):
    rows = s.shape[0]
    iota = lax.broadcasted_iota(jnp.int32, s.shape, 0).astype(F32)
    vals, picks = [], []
    for _ in range(n):
        m = jnp.max(s, axis=0, keepdims=True)
        pos = jnp.min(jnp.where(s == m, iota, float(rows)), axis=0, keepdims=True)
        hit = iota == pos
        vals.append(m)
        picks.append(pos if payload is None else jnp.max(jnp.where(hit, payload, -1.0), axis=0, keepdims=True))
        s = jnp.where(hit, -jnp.inf, s)
    return jnp.concatenate(vals, axis=0), jnp.concatenate(picks, axis=0)


_CAND_WIDTHS = [PEER_TOPK // (a + 1) for a in range(PEER_TOPK)]


def _route_kernel(x_ref, sbo_ref, swo_ref, gsb_ref, gsw_ref, wout_ref, gffn_ref, wq_ref,
                  k1h_ref, k1l_ref, k2h_ref, k2l_ref, x1_ref, h2_ref, idx_ref, gate_ref):
    sbn = _rms(sbo_ref[...], gsb_ref[...]).astype(BF16)
    swn = _rms(swo_ref[...], gsw_ref[...]).astype(BF16)
    x1 = x_ref[...] + _dot(sbn, wout_ref[0:SB_WIDTH, :]) + _dot(swn, wout_ref[SB_WIDTH:, :])
    x1_ref[...] = x1
    h2 = _rms(x1, gffn_ref[...])
    h2_ref[...] = h2
    q = _dot(h2.astype(BF16), wq_ref[...])

    def scores(qq, kh_ref, kl_ref, h):
        qh, ql = _split_bf16(qq)
        return _dot_t(kh_ref[h], qh) + _dot_t(kh_ref[h], ql) + _dot_t(kl_ref[h], qh)

    idx_rows, gate_rows = [], []
    for h in range(PEER_HEADS):
        c0 = h * 2 * PEER_HALF
        v1, i1 = _topk_rows(scores(q[:, c0:c0 + PEER_HALF], k1h_ref, k1l_ref, h), PEER_TOPK)
        v2, i2 = _topk_rows(scores(q[:, c0 + PEER_HALF:c0 + 2 * PEER_HALF], k2h_ref, k2l_ref, h), PEER_TOPK)
        pad = jnp.full((-sum(_CAND_WIDTHS) % SUBLANES, q.shape[0]), -jnp.inf, F32)
        cand = jnp.concatenate([v1[a:a + 1] + v2[:n] for a, n in enumerate(_CAND_WIDTHS)] + [pad], axis=0)
        cidx = jnp.concatenate([i1[a:a + 1] * float(N_KEYS) + i2[:n] for a, n in enumerate(_CAND_WIDTHS)] + [pad],
                               axis=0)
        top_s, idx = _topk_rows(cand, PEER_TOPK, payload=cidx)
        e = jnp.exp(top_s - top_s[0:1])
        idx_rows.append(idx)
        gate_rows.append(e / jnp.sum(e, axis=0, keepdims=True))
    idx_t = jnp.concatenate(idx_rows, axis=0)
    idx_ref[...] = idx_t.T.astype(jnp.int32)
    gate_ref[...] = jnp.concatenate(gate_rows, axis=0).T


def _route(x2, sbo, swo, gsb, gsw, wout, gffn, wq, k1h, k1l, k2h, k2l):
    T = x2.shape[0]
    row = lambda i: (i, 0)
    fixed2 = lambda i: (0, 0)
    fixed3 = lambda i: (0, 0, 0)
    keys = pl.BlockSpec(k1h.shape, fixed3)
    return pl.pallas_call(
        _route_kernel,
        grid=(T // ROUTE_TILE,),
        in_specs=[pl.BlockSpec((ROUTE_TILE, D_MODEL), row), pl.BlockSpec((ROUTE_TILE, SB_WIDTH), row),
                  pl.BlockSpec((ROUTE_TILE, SWA_Q_WIDTH), row),
                  pl.BlockSpec((1, SB_WIDTH), fixed2), pl.BlockSpec((1, SWA_Q_WIDTH), fixed2),
                  pl.BlockSpec(wout.shape, fixed2), pl.BlockSpec((1, D_MODEL), fixed2),
                  pl.BlockSpec(wq.shape, fixed2), keys, keys, keys, keys],
        out_specs=[pl.BlockSpec((ROUTE_TILE, D_MODEL), row), pl.BlockSpec((ROUTE_TILE, D_MODEL), row),
                   pl.BlockSpec((ROUTE_TILE, SLOTS), row), pl.BlockSpec((ROUTE_TILE, SLOTS), row)],
        out_shape=[jax.ShapeDtypeStruct((T, D_MODEL), F32), jax.ShapeDtypeStruct((T, D_MODEL), F32),
                   jax.ShapeDtypeStruct((T, SLOTS), jnp.int32), jax.ShapeDtypeStruct((T, SLOTS), F32)],
        compiler_params=pltpu.CompilerParams(vmem_limit_bytes=VMEM_LIMIT, dimension_semantics=("arbitrary",)),
        name="route",
    )(x2, sbo, swo, gsb, gsw, wout, gffn, wq, k1h, k1l, k2h, k2l)


def _tile_table(tab):
    return tab.astype(BF16).reshape(N_EXPERTS, ROW_CHUNKS, LANES)


def _gather_tiles(idx_ref, base, tab_ref, g_ref):
    for k in range(SLOTS):
        g_ref[pl.ds(k * ROW_CHUNKS, ROW_CHUNKS), :] = tab_ref[idx_ref[base + k]]


def _index_copy(idx_hbm, group, ibuf, sem):
    start = pl.multiple_of(group * GROUP_IDX, GROUP_IDX)
    return pltpu.make_async_copy(idx_hbm.at[pl.ds(start, GROUP_IDX)], ibuf, sem)


def _for_each_group(idx_hbm, ibufs, sems, process):
    step = pl.program_id(0)
    n_groups = pl.num_programs(0) * GROUPS_PER_STEP

    @pl.when(step == 0)
    def _():
        _index_copy(idx_hbm, 0, ibufs[0], sems.at[0]).start()

    def body(p, carry):
        for slot in range(2):
            local = 2 * p + slot
            group = step * GROUPS_PER_STEP + local
            _index_copy(idx_hbm, group, ibufs[slot], sems.at[slot]).wait()

            @pl.when(group + 1 < n_groups)
            def _():
                _index_copy(idx_hbm, group + 1, ibufs[1 - slot], sems.at[1 - slot]).start()

            process(pl.multiple_of(local * TOK_GROUP, TOK_GROUP), ibufs[slot])
        return carry

    lax.fori_loop(0, GROUPS_PER_STEP // 2, body, 0)


def _block_diag_mask():
    c = lax.broadcasted_iota(jnp.int32, (ROW_CHUNKS, GROWS), 0)
    j = lax.broadcasted_iota(jnp.int32, (ROW_CHUNKS, GROWS), 1)
    return (j % ROW_CHUNKS) == c


def _split_rows(v):
    hi, lo = _split_bf16(v)
    return jnp.concatenate([hi, lo], axis=0)


def _fold_rows(v):
    m = v.shape[0] // 2
    return v[:m] + v[m:]


def _down_kernel(idx_hbm, h2t_ref, gate_ref, tab_ref, w_ref, g0, g1, i0, i1, sems):
    gbufs = (g0, g1)
    diag = _block_diag_mask()
    j = lax.broadcasted_iota(jnp.int32, (GROWS, SLOTS), 0)
    k = lax.broadcasted_iota(jnp.int32, (GROWS, SLOTS), 1)
    pool = jnp.where(j // ROW_CHUNKS == k, 1.0, 0.0).astype(BF16)

    def process(t0, ibuf):
        rows = []
        for u in range(TOK_GROUP):
            g = gbufs[u % 2]
            _gather_tiles(ibuf, u * SLOTS, tab_ref, g)
            z = _fold_rows(_dot_t(_split_rows(h2t_ref[t0 + u]), g[...]))
            rows.append(jnp.sum(jnp.where(diag, z, 0.0), axis=0, keepdims=True))
        hid = _fold_rows(_dot(_split_rows(jnp.concatenate(rows, axis=0)), pool))
        gelu = 0.5 * hid * (1.0 + lax.erf(hid * (2.0 ** -0.5)))
        w_ref[pl.ds(t0, TOK_GROUP), :] = gate_ref[pl.ds(t0, TOK_GROUP), :] * gelu

    _for_each_group(idx_hbm, (i0, i1), sems, process)


def _up_kernel(idx_hbm, w_ref, x1t_ref, gfin_ref, tab_ref, y_ref, g0, g1, i0, i1, sems):
    gbufs = (g0, g1)
    diag = _block_diag_mask()
    k = lax.broadcasted_iota(jnp.int32, (SLOTS, GROWS), 0)
    j = lax.broadcasted_iota(jnp.int32, (SLOTS, GROWS), 1)
    spread = jnp.where(j // ROW_CHUNKS == k, 1.0, 0.0).astype(BF16)

    def process(t0, ibuf):
        wrep = _fold_rows(_dot(_split_rows(w_ref[pl.ds(t0, TOK_GROUP), :]), spread))
        for u in range(TOK_GROUP):
            g = gbufs[u % 2]
            _gather_tiles(ibuf, u * SLOTS, tab_ref, g)
            wexp = jnp.where(diag, jnp.broadcast_to(wrep[u:u + 1], (ROW_CHUNKS, GROWS)), 0.0)
            x2 = x1t_ref[t0 + u] + _fold_rows(_dot(_split_rows(wexp), g[...]))
            ms = jnp.sum(jnp.sum(x2 * x2, axis=1, keepdims=True), axis=0, keepdims=True) * (1.0 / D_MODEL)
            y_ref[t0 + u] = x2 * lax.rsqrt(ms + EPS) * gfin_ref[...]

    _for_each_group(idx_hbm, (i0, i1), sems, process)


def _gather_specs():
    idx_spec = pl.BlockSpec(memory_space=pl.ANY)
    tab_spec = pl.BlockSpec((N_EXPERTS, ROW_CHUNKS, LANES), lambda i: (0, 0, 0), pipeline_mode=pl.Buffered(1))
    scratch = ([pltpu.VMEM((GROWS, LANES), BF16)] * 2 + [pltpu.SMEM((GROUP_IDX,), jnp.int32)] * 2
               + [pltpu.SemaphoreType.DMA((2,))])
    params = pltpu.CompilerParams(vmem_limit_bytes=VMEM_LIMIT, dimension_semantics=("arbitrary",))
    return idx_spec, tab_spec, scratch, params


def _peer_down(idx_flat, h2t, gate, tab):
    T = h2t.shape[0]
    row = lambda i: (i, 0)
    tok = pl.BlockSpec((TOK_TILE, ROW_CHUNKS, LANES), lambda i: (i, 0, 0))
    idx_spec, tab_spec, scratch, params = _gather_specs()
    return pl.pallas_call(
        _down_kernel,
        grid=(T // TOK_TILE,),
        in_specs=[idx_spec, tok, pl.BlockSpec((TOK_TILE, SLOTS), row), tab_spec],
        out_specs=pl.BlockSpec((TOK_TILE, SLOTS), row),
        out_shape=jax.ShapeDtypeStruct((T, SLOTS), F32),
        scratch_shapes=scratch, compiler_params=params, name="peer_down",
    )(idx_flat, h2t, gate, tab)


def _peer_up(idx_flat, w, x1t, gfin_t, tab):
    T = x1t.shape[0]
    tok = pl.BlockSpec((TOK_TILE, ROW_CHUNKS, LANES), lambda i: (i, 0, 0))
    idx_spec, tab_spec, scratch, params = _gather_specs()
    return pl.pallas_call(
        _up_kernel,
        grid=(T // TOK_TILE,),
        in_specs=[idx_spec, pl.BlockSpec((TOK_TILE, SLOTS), lambda i: (i, 0)), tok,
                  pl.BlockSpec((ROW_CHUNKS, LANES), lambda i: (0, 0)), tab_spec],
        out_specs=tok,
        out_shape=jax.ShapeDtypeStruct((T, ROW_CHUNKS, LANES), F32),
        scratch_shapes=scratch, compiler_params=params, name="peer_up",
    )(idx_flat, w, x1t, gfin_t, tab)


def _rope_table():
    half = HEAD_DIM // 2
    inv_freq = ROPE_THETA ** (-jnp.arange(half, dtype=F32) / half)
    lane = jnp.arange(LANES)
    sign = jnp.where(lane % HEAD_DIM < half, -1.0, 1.0).astype(F32)
    return jnp.stack([inv_freq[lane % half], sign])


def _extend_w_in(w):
    c0 = 3 * SB_WIDTH + SWA_Q_WIDTH
    kv = [w[:, c0 + n * HEAD_DIM:c0 + (n + 1) * HEAD_DIM] for n in range(2 * SWA_KV_HEADS)]
    return jnp.concatenate([w[:, :c0]] + [kv[n] for n in (0, 0, 1, 1, 2, 2, 3, 3)], axis=1).astype(BF16)


def kernel(x, positions, attn_norm, w_in, sb_out_norm, swa_sinks, swa_out_norm, w_out, ffn_norm, peer_w_query, peer_sub_keys_1, peer_sub_keys_2, peer_expert_down, peer_expert_up, final_norm):
    B, S, D = x.shape
    T = B * S
    assert w_in.shape[0] == 1, "single-layer trunk: the final norm is fused into the last kernel"
    x2 = x.reshape(T, D)
    sbq, sbk, sbv, swq, swk, swv = _proj(x2, positions.reshape(T, 1), attn_norm[0].reshape(1, D),
                                         _extend_w_in(w_in[0]), _rope_table())
    sbo = _sb_attention(sbq.reshape(B, S, -1), sbk.reshape(B, S, -1), sbv.reshape(B, S, -1))
    swo = _swa_attention(swa_sinks[0].reshape(1, -1), swq.reshape(B, S, -1), swk.reshape(B, S, -1),
                         swv.reshape(B, S, -1))
    k1h, k1l = _split_bf16(peer_sub_keys_1[0])
    k2h, k2l = _split_bf16(peer_sub_keys_2[0])
    x1, h2, idx, gate = _route(x2, sbo.reshape(T, -1), swo.reshape(T, -1), sb_out_norm[0].reshape(1, -1),
                               swa_out_norm[0].reshape(1, -1), w_out[0].astype(BF16), ffn_norm[0].reshape(1, D),
                               peer_w_query[0].astype(BF16), k1h, k1l, k2h, k2l)
    idx_flat = idx.reshape(T * SLOTS)
    tiles = (T, ROW_CHUNKS, LANES)
    w = _peer_down(idx_flat, h2.reshape(tiles), gate, _tile_table(peer_expert_down[0]))
    y = _peer_up(idx_flat, w, x1.reshape(tiles), final_norm.reshape(ROW_CHUNKS, LANES),
                 _tile_table(peer_expert_up[0]))
    return y.reshape(B, S, D)
```
